```python
import jax, jax.numpy as jnp
from jax import lax
import numpy as np

D_MODEL = 4096
BATCH = 4
SEQ = 2048
DEPTH = 2

GRID_W = 64
CTX_LEN = 256
N_MIXERS = 2
N_LAYERS_A = (DEPTH + 1) // 2
N_LAYERS_B = DEPTH // 2
NA_HEADS = 32
NA_HEAD_DIM = D_MODEL // NA_HEADS
NA_KH_MAX = 8
NA_KW = 16
RPB_H = 2 * NA_KH_MAX - 1
RPB_W = 2 * NA_KW - 1
SG_CHUNK = 128
SG_WIDTH = D_MODEL
SG_GROUPS = 8
SG_GROUP_DIM = SG_WIDTH // SG_GROUPS
D_FF = 2 * D_MODEL
N_FFN = 2
N_SUB = 3
N_MOD = 3 * N_SUB
EPS = 1e-6
NEG_INF = -1e30

kernel_name = "hybrid_natten_gmlp_macaron_dit"


def rms_norm(x, gain):
    xf = x.astype(jnp.float32)
    y = xf * lax.rsqrt(jnp.mean(xf * xf, axis=-1, keepdims=True) + EPS)
    return (y * gain.astype(jnp.float32)).astype(x.dtype)


def modulate(x, gain, mod, j):
    return rms_norm(x, gain) * (1 + mod[:, 3 * j + 1]) + mod[:, 3 * j]


def swiglu(h, w_in, w_out):
    a, g = jnp.split(h @ w_in, 2, axis=-1)
    return (jax.nn.silu(a) * g) @ w_out


def half_ffn(x, gain, mod, j, w_in, w_out):
    h = modulate(x, gain, mod, j)
    return x + 0.5 * mod[:, 3 * j + 2] * swiglu(h, w_in, w_out)


def neighbourhood_attention(h_lat, h_ctx, w_qkv, w_o, q_gain, k_gain, rpb, with_ctx_out):
    B, S, _ = h_lat.shape
    rows = S // GRID_W
    kh = min(NA_KH_MAX, rows)
    scale = NA_HEAD_DIM ** -0.5

    def project(h):
        qkv = (h @ w_qkv).reshape(h.shape[0], h.shape[1], 3, NA_HEADS, NA_HEAD_DIM)
        q = rms_norm(qkv[:, :, 0], q_gain) * scale
        k = rms_norm(qkv[:, :, 1], k_gain)
        return q, k, qkv[:, :, 2]

    q, k, v = project(h_lat)
    qc, kc, vc = project(h_ctx)

    def to_grid(t):
        return t.reshape(B, rows, GRID_W, NA_HEADS, NA_HEAD_DIM).transpose(1, 0, 3, 2, 4)

    q_g, k_g, v_g = to_grid(q), to_grid(k), to_grid(v)

    col = jnp.arange(GRID_W)
    cs = jnp.clip(col - NA_KW // 2, 0, GRID_W - NA_KW)
    col_ok = (col[None, :] >= cs[:, None]) & (col[None, :] < cs[:, None] + NA_KW)
    dc = jnp.clip(col[None, :] - col[:, None], -(NA_KW - 1), NA_KW - 1) + NA_KW - 1
    rpb_cols = rpb[:, :, dc]
    r_all = jnp.arange(rows)
    rs = jnp.clip(r_all - kh // 2, 0, rows - kh)
    n_lat = kh * GRID_W

    def row_block(args):
        q_r, r, r0 = args
        k_r = lax.dynamic_slice_in_dim(k_g, r0, kh, axis=0)
        v_r = lax.dynamic_slice_in_dim(v_g, r0, kh, axis=0)
        dr = r0 + jnp.arange(kh) - r + NA_KH_MAX - 1
        bias = jnp.take(rpb_cols, dr, axis=1).transpose(0, 2, 1, 3)
        s_lat = jnp.einsum('bhqd,ibhkd->bhqik', q_r, k_r).astype(jnp.float32)
        s_lat = s_lat + bias[None].astype(jnp.float32)
        s_lat = jnp.where(col_ok[:, None, :], s_lat, NEG_INF)
        s_ctx = jnp.einsum('bhqd,bchd->bhqc', q_r, kc).astype(jnp.float32)
        s = jnp.concatenate([s_lat.reshape(B, NA_HEADS, GRID_W, n_lat), s_ctx], axis=-1)
        p = jax.nn.softmax(s, axis=-1).astype(v.dtype)
        p_lat = p[..., :n_lat].reshape(B, NA_HEADS, GRID_W, kh, GRID_W)
        o = (jnp.einsum('bhqik,ibhkd->bqhd', p_lat, v_r)
             + jnp.einsum('bhqc,bchd->bqhd', p[..., n_lat:], vc))
        return o

    o_lat = lax.map(row_block, (q_g, r_all, rs))
    o_lat = o_lat.transpose(1, 0, 2, 3, 4).reshape(B, S, D_MODEL) @ w_o

    o_ctx = None
    if with_ctx_out:
        C = h_ctx.shape[1]
        s_c = jnp.einsum('bqhd,bkhd->bhqk', qc, kc).astype(jnp.float32)
        p_c = jax.nn.softmax(s_c, axis=-1).astype(vc.dtype)
        o_ctx = jnp.einsum('bhqk,bkhd->bqhd', p_c, vc).reshape(B, C, D_MODEL) @ w_o
    return o_lat, o_ctx


def spatial_gating(h, w_in, b_in, v_gain, w_s, b_s, w_out):
    B, L, _ = h.shape
    u, v = jnp.split(jax.nn.gelu(h @ w_in + b_in), 2, axis=-1)
    v = rms_norm(v, v_gain).reshape(B, L // SG_CHUNK, SG_CHUNK, SG_GROUPS, SG_GROUP_DIM)
    mixed = jnp.einsum('gpq,bnqgc->bnpgc', w_s, v) + b_s.T[None, None, :, :, None]
    return (u * mixed.reshape(B, L, SG_WIDTH)) @ w_out


def setup_inputs(seed: int = 0) -> dict:
    key = jax.random.key(seed)
    ks = jax.random.split(key, 20)

    def nrm(k, shape, s):
        return jax.random.normal(k, shape, jnp.float32) * s

    D = D_MODEL
    return {
        "x": nrm(ks[0], (BATCH, SEQ, D), 1.0),
        "c": nrm(ks[1], (BATCH, D), 1.0),
        "ctx": nrm(ks[2], (BATCH, CTX_LEN, D), 1.0),
        "c_ctx": nrm(ks[3], (D,), 1.0),
        "w_ada": nrm(ks[4], (DEPTH, D, N_MOD * D), 0.5 * D ** -0.5),
        "b_ada": nrm(ks[5], (DEPTH, N_MOD * D), 0.02),
        "norm_g": 1.0 + nrm(ks[6], (DEPTH, N_SUB, D), 0.02),
        "ffn_w_in": nrm(ks[7], (DEPTH, N_FFN, D, 2 * D_FF), D ** -0.5),
        "ffn_w_out": nrm(ks[8], (DEPTH, N_FFN, D_FF, D), D_FF ** -0.5),
        "na_w_qkv": nrm(ks[9], (N_LAYERS_A, D, 3 * D), D ** -0.5),
        "na_w_o": nrm(ks[10], (N_LAYERS_A, D, D), D ** -0.5),
        "na_q_gain": 1.0 + nrm(ks[11], (N_LAYERS_A, NA_HEAD_DIM), 0.02),
        "na_k_gain": 1.0 + nrm(ks[12], (N_LAYERS_A, NA_HEAD_DIM), 0.02),
        "na_rpb": nrm(ks[13], (N_LAYERS_A, NA_HEADS, RPB_H, RPB_W), 0.5),
        "sg_w_in": nrm(ks[14], (N_LAYERS_B, D, 2 * SG_WIDTH), D ** -0.5),
        "sg_b_in": nrm(ks[15], (N_LAYERS_B, 2 * SG_WIDTH), 0.02),
        "sg_v_gain": 1.0 + nrm(ks[16], (N_LAYERS_B, SG_WIDTH), 0.02),
        "sg_w_s": nrm(ks[17], (N_LAYERS_B, SG_GROUPS, SG_CHUNK, SG_CHUNK), SG_CHUNK ** -0.5),
        "sg_b_s": 1.0 + nrm(ks[18], (N_LAYERS_B, SG_GROUPS, SG_CHUNK), 0.02),
        "sg_w_out": nrm(ks[19], (N_LAYERS_B, SG_WIDTH, D), SG_WIDTH ** -0.5),
    }


def reference(x, c, ctx, c_ctx, w_ada, b_ada, norm_g, ffn_w_in, ffn_w_out,
              na_w_qkv, na_w_o, na_q_gain, na_k_gain, na_rpb,
              sg_w_in, sg_b_in, sg_v_gain, sg_w_s, sg_b_s, sg_w_out):
    B = x.shape[0]
    z = ctx
    for i in range(DEPTH):
        kind = i % N_MIXERS
        j = i // N_MIXERS
        keep_ctx = i < DEPTH - 1
        ctx_in = keep_ctx or kind == 0
        m_x = (jax.nn.silu(c) @ w_ada[i] + b_ada[i]).reshape(B, N_MOD, 1, D_MODEL)
        m_z = (jax.nn.silu(c_ctx)[None] @ w_ada[i] + b_ada[i]).reshape(1, N_MOD, 1, D_MODEL)

        x = half_ffn(x, norm_g[i, 0], m_x, 0, ffn_w_in[i, 0], ffn_w_out[i, 0])
        if ctx_in:
            z = half_ffn(z, norm_g[i, 0], m_z, 0, ffn_w_in[i, 0], ffn_w_out[i, 0])

        hx = modulate(x, norm_g[i, 1], m_x, 1)
        if kind == 0:
            hz = modulate(z, norm_g[i, 1], m_z, 1)
            ox, oz = neighbourhood_attention(hx, hz, na_w_qkv[j], na_w_o[j], na_q_gain[j],
                                             na_k_gain[j], na_rpb[j], keep_ctx)
        else:
            sg = (sg_w_in[j], sg_b_in[j], sg_v_gain[j], sg_w_s[j], sg_b_s[j], sg_w_out[j])
            ox = spatial_gating(hx, *sg)
            oz = spatial_gating(modulate(z, norm_g[i, 1], m_z, 1), *sg) if keep_ctx else None
        x = x + m_x[:, 5] * ox

        x = half_ffn(x, norm_g[i, 2], m_x, 2, ffn_w_in[i, 1], ffn_w_out[i, 1])
        if keep_ctx:
            z = z + m_z[:, 5] * oz
            z = half_ffn(z, norm_g[i, 2], m_z, 2, ffn_w_in[i, 1], ffn_w_out[i, 1])
    return x
```

```python
import functools

import jax
import jax.numpy as jnp
from jax import lax
from jax.experimental import pallas as pl
from jax.experimental.pallas import tpu as pltpu

D_MODEL = 4096
GRID_W = 64
NA_HEADS = 32
NA_HEAD_DIM = 128
NA_KH = 8
NA_KW = 16
RPB_H = 15
RPB_W = 31
SG_CHUNK = 128
SG_GROUPS = 8
SG_GROUP_DIM = D_MODEL // SG_GROUPS
N_MOD = 9
EPS = 1e-6
NEG_INF = -1e30

V7X_VMEM_BYTES = 64 * 1024 * 1024
V7X_LANES = 128
VMEM_LIMIT = V7X_VMEM_BYTES - 4 * 1024 * 1024

Q_ROWS = 4
KEY_ROWS = 12
N_QGROUPS = 32 // Q_ROWS
BIAS_VARIANTS = 3

F32 = jnp.float32
BF16 = jnp.bfloat16


def _params(sem, vmem=VMEM_LIMIT):
    return pltpu.CompilerParams(dimension_semantics=sem, vmem_limit_bytes=vmem)


def _sigmoid(a):
    return 1.0 / (1.0 + jnp.exp(-a))


def _ada_body(c_ref, w_ref, b_ref, o_ref):
    cv = c_ref[...]
    a = cv * _sigmoid(cv)
    o_ref[...] = jnp.dot(a, w_ref[...], preferred_element_type=F32) + b_ref[...]


def ada_mods(c8, w_ada, b_ada, tn=1024):
    n_layers, d, n = w_ada.shape
    rows = c8.shape[0]
    return pl.pallas_call(
        _ada_body,
        grid=(n_layers, n // tn),
        in_specs=[
            pl.BlockSpec((rows, d), lambda l, j: (0, 0)),
            pl.BlockSpec((None, d, tn), lambda l, j: (l, 0, j)),
            pl.BlockSpec((None, 1, tn), lambda l, j: (l, 0, j)),
        ],
        out_specs=pl.BlockSpec((None, rows, tn), lambda l, j: (l, 0, j)),
        out_shape=jax.ShapeDtypeStruct((n_layers, rows, n), F32),
        compiler_params=_params(("arbitrary", "arbitrary")),
        name="ada_mods",
    )(c8, w_ada, b_ada.reshape(n_layers, 1, n))


def _mod_spec(layer, k, row_of_block, grid_rank):
    if grid_rank == 1:
        imap = lambda i: (layer, row_of_block(i), k, 0, 0)
    else:
        imap = lambda j, i: (layer, row_of_block(i), k, 0, 0)
    return pl.BlockSpec((None, None, None, 1, D_MODEL), imap)


def _mod_col_spec(layer, k, row_of_block, tn):
    return pl.BlockSpec((None, None, None, 1, tn), lambda j, i: (layer, row_of_block(i), k, 0, j))


def _norm_body(n_first, *refs):
    if n_first is None:
        x_ref, g_ref, sh_ref, sc_ref, o_ref = refs
        xv = x_ref[...]
    else:
        xa_ref, xb_ref, g_ref, sh_ref, sc_ref, o_ref = refs
        xv = jnp.where(pl.program_id(0) < n_first, xa_ref[...], xb_ref[...])
    ms = jnp.mean(xv * xv, axis=-1, keepdims=True)
    y = xv * lax.rsqrt(ms + EPS) * g_ref[...]
    o_ref[...] = (y * (1.0 + sc_ref[...]) + sh_ref[...]).astype(o_ref.dtype)


def norm_modulate(xs, n_rows, gain, mods, layer, sub, row_of_block, tm=512, row_off=0):
    d = D_MODEL
    nb = n_rows // tm
    off = row_off // tm
    if len(xs) == 1:
        n_first = None
        x_specs = [pl.BlockSpec((tm, d), lambda i: (i + off, 0))]
    else:
        n_first = xs[0].shape[0] // tm
        x_specs = [
            pl.BlockSpec((tm, d), lambda i: (jnp.minimum(i, n_first - 1), 0)),
            pl.BlockSpec((tm, d), lambda i: (jnp.maximum(i - n_first, 0), 0)),
        ]
    return pl.pallas_call(
        functools.partial(_norm_body, n_first),
        grid=(nb,),
        in_specs=x_specs + [
            pl.BlockSpec((1, d), lambda i: (0, 0)),
            _mod_spec(layer, 3 * sub, row_of_block, 1),
            _mod_spec(layer, 3 * sub + 1, row_of_block, 1),
        ],
        out_specs=pl.BlockSpec((tm, d), lambda i: (i, 0)),
        out_shape=jax.ShapeDtypeStruct((n_rows, d), BF16),
        compiler_params=_params(("arbitrary",)),
        name="norm_modulate",
    )(*xs, gain.reshape(1, d), mods, mods)


def _mm_body(n_w, n_extra, epilogue, *refs):
    h_ref = refs[0]
    w_refs = refs[1:1 + n_w]
    extra = refs[1 + n_w:1 + n_w + n_extra]
    outs = refs[1 + n_w + n_extra:-1]
    wb_ref = refs[-1]

    @pl.when(pl.program_id(1) == 0)
    def _():
        for t in range(n_w):
            wb_ref[t] = w_refs[t][...].astype(BF16)

    h = h_ref[...]
    ys = [jnp.dot(h, wb_ref[t], preferred_element_type=F32) for t in range(n_w)]
    epilogue(ys, extra, outs)


def fused_matmul(h, n_rows, w, w_prefix, col_offsets, n_cols, tm, tn, epilogue, extras, outs,
                 name, h_row_off=0):
    k = h.shape[1]
    n_w = len(col_offsets)
    h_off = h_row_off // tm
    none_prefix = (None,) * len(w_prefix)
    w_specs = [
        pl.BlockSpec(none_prefix + (k, tn),
                     lambda j, i, o=o // tn: tuple(w_prefix) + (0, j + o))
        for o in col_offsets
    ]
    return pl.pallas_call(
        functools.partial(_mm_body, n_w, len(extras), epilogue),
        grid=(n_cols // tn, n_rows // tm),
        in_specs=[pl.BlockSpec((tm, k), lambda j, i: (i + h_off, 0))] + w_specs
                 + [s for _, s in extras],
        out_specs=[s for _, s in outs],
        out_shape=[a for a, _ in outs],
        scratch_shapes=[pltpu.VMEM((n_w, k, tn), BF16)],
        compiler_params=_params(("arbitrary", "arbitrary")),
        name=name,
    )(h, *([w] * n_w), *[a for a, _ in extras])


def _swiglu_epilogue(ys, extra, outs):
    a, g = ys
    outs[0][...] = (a * _sigmoid(a) * g).astype(outs[0].dtype)


def _gelu_tanh(v):
    return 0.5 * v * (1.0 + jnp.tanh(0.7978845608028654 * (v + 0.044715 * (v * v * v))))


def _sg_in_epilogue(ys, extra, outs):
    bu_ref, bv_ref = extra
    outs[0][...] = _gelu_tanh(ys[0] + bu_ref[...]).astype(outs[0].dtype)
    outs[1][...] = _gelu_tanh(ys[1] + bv_ref[...]).astype(outs[1].dtype)


def _head_rms(y, gain, mult):
    pieces = []
    for hh in range(y.shape[1] // NA_HEAD_DIM):
        seg = y[:, hh * NA_HEAD_DIM:(hh + 1) * NA_HEAD_DIM]
        ms = jnp.mean(seg * seg, axis=-1, keepdims=True)
        pieces.append(seg * lax.rsqrt(ms + EPS) * gain * mult)
    return pieces


def _qkv_epilogue(ys, extra, outs):
    qg_ref, kg_ref = extra
    q_ref, k_ref, v_ref = outs
    scale = NA_HEAD_DIM ** -0.5
    for hh, seg in enumerate(_head_rms(ys[0], qg_ref[...], scale)):
        q_ref[:, hh * NA_HEAD_DIM:(hh + 1) * NA_HEAD_DIM] = seg.astype(q_ref.dtype)
    for hh, seg in enumerate(_head_rms(ys[1], kg_ref[...], 1.0)):
        k_ref[:, hh * NA_HEAD_DIM:(hh + 1) * NA_HEAD_DIM] = seg.astype(k_ref.dtype)
    v_ref[...] = ys[2].astype(v_ref.dtype)


def _residual_epilogue(coef, n_first, ys, extra, outs):
    if n_first is None:
        gate_ref, x_ref = extra
        xv = x_ref[...]
    else:
        gate_ref, xa_ref, xb_ref = extra
        xv = jnp.where(pl.program_id(1) < n_first, xa_ref[...], xb_ref[...])
    outs[0][...] = xv + (coef * gate_ref[...]) * ys[0]


def residual_matmul(h, n_rows, w, w_prefix, x_olds, x_row_off, mods, layer, gate_k, row_of_block,
                    coef, tm, tn, name, h_row_off=0):
    n = D_MODEL
    if len(x_olds) == 1:
        n_first = None
        xo = x_row_off // tm
        x_specs = [pl.BlockSpec((tm, tn), lambda j, i: (i + xo, j))]
    else:
        n_first = x_olds[0].shape[0] // tm
        x_specs = [
            pl.BlockSpec((tm, tn), lambda j, i: (jnp.minimum(i, n_first - 1), j)),
            pl.BlockSpec((tm, tn), lambda j, i: (jnp.maximum(i - n_first, 0), j)),
        ]
    extras = [(mods, _mod_col_spec(layer, gate_k, row_of_block, tn))]
    extras += list(zip(x_olds, x_specs))
    outs = [(jax.ShapeDtypeStruct((n_rows, n), F32), pl.BlockSpec((tm, tn), lambda j, i: (i, j)))]
    return fused_matmul(h, n_rows, w, w_prefix, [0], n, tm, tn,
                        functools.partial(_residual_epilogue, coef, n_first), extras, outs, name,
                        h_row_off=h_row_off)[0]


def ffn_in(h, n_rows, w_in, w_prefix, tm=512, tn=512, h_row_off=0):
    d_ff = w_in.shape[-1] // 2
    outs = [(jax.ShapeDtypeStruct((n_rows, d_ff), BF16), pl.BlockSpec((tm, tn), lambda j, i: (i, j)))]
    return fused_matmul(h, n_rows, w_in, w_prefix, [0, d_ff], d_ff, tm, tn, _swiglu_epilogue, [],
                        outs, "ffn_in", h_row_off=h_row_off)[0]


def half_ffn(x_olds, n_rows, x_row_off, mods, layer, sub, row_fn, gain, w_in, w_out, f):
    h = norm_modulate(x_olds, n_rows, gain, mods, layer, sub, row_fn(512), row_off=x_row_off)
    act = ffn_in(h, n_rows, w_in, (layer, f))
    return residual_matmul(act, n_rows, w_out, (layer, f), x_olds, x_row_off, mods, layer,
                           3 * sub + 2, row_fn(256), 0.5, 256, 512, "ffn_out")


def _bias_body(rpb_ref, o_ref):
    hd = pl.program_id(0)
    qc = lax.broadcasted_iota(jnp.int32, (GRID_W, V7X_LANES), 0)
    lane = lax.broadcasted_iota(jnp.int32, (GRID_W, V7X_LANES), 1)
    kc = lane & (GRID_W - 1)
    dc = jnp.clip(kc - qc, -(NA_KW - 1), NA_KW - 1) + NA_KW - 1
    cs = jnp.clip(qc - NA_KW // 2, 0, GRID_W - NA_KW)
    col_ok = (kc >= cs) & (kc < cs + NA_KW)
    left = lane < GRID_W
    neg = jnp.full((GRID_W, V7X_LANES), NEG_INF, F32)

    cw = [jnp.zeros((GRID_W, V7X_LANES), F32) for _ in range(RPB_H)]
    for d in range(RPB_W):
        hit = dc == d
        for dr in range(RPB_H):
            cw[dr] = jnp.where(hit, rpb_ref[hd * (RPB_H * RPB_W) + dr * RPB_W + d], cw[dr])
    cw = [jnp.where(col_ok, t, neg) for t in cw]

    def block(variant, qr, kr):
        if variant == 0:
            ok, dr = 0 <= kr < NA_KH, kr - qr + NA_KH - 1
        elif variant == 1:
            ok, dr = 0 <= kr - qr < NA_KH, kr - qr + NA_KH - 1 - Q_ROWS
        else:
            ok, dr = KEY_ROWS - NA_KH <= kr < KEY_ROWS, kr - qr - 1
        return cw[dr] if ok else None

    for variant in range(BIAS_VARIANTS):
        for qr in range(Q_ROWS):
            for p in range(KEY_ROWS // 2):
                b0, b1 = block(variant, qr, 2 * p), block(variant, qr, 2 * p + 1)
                if b0 is None and b1 is None:
                    blk = neg
                else:
                    blk = jnp.where(left, neg if b0 is None else b0, neg if b1 is None else b1)
                o_ref[variant, qr * GRID_W:(qr + 1) * GRID_W, p * V7X_LANES:(p + 1) * V7X_LANES] = blk


def attention_bias(rpb):
    nq, nk = Q_ROWS * GRID_W, KEY_ROWS * GRID_W
    return pl.pallas_call(
        _bias_body,
        grid=(NA_HEADS,),
        in_specs=[pl.BlockSpec(memory_space=pltpu.SMEM)],
        out_specs=pl.BlockSpec((None, BIAS_VARIANTS, nq, nk), lambda h: (h, 0, 0, 0)),
        out_shape=jax.ShapeDtypeStruct((NA_HEADS, BIAS_VARIANTS, nq, nk), F32),
        compiler_params=_params(("arbitrary",)),
        name="attention_bias",
    )(rpb.reshape(-1))


_NT_DIMS = (((1,), (1,)), ((), ()))


def _softmax_av(s_parts, v_parts):
    m = functools.reduce(jnp.maximum, [jnp.max(s, axis=-1, keepdims=True) for s in s_parts])
    ps = [jnp.exp(s - m) for s in s_parts]
    l = functools.reduce(lambda a, b: a + b, [jnp.sum(p, axis=-1, keepdims=True) for p in ps])
    o = functools.reduce(lambda a, b: a + b, [
        jnp.dot(p.astype(BF16), v, preferred_element_type=F32) for p, v in zip(ps, v_parts)])
    return o / l


def _na_body(q_ref, k_ref, v_ref, kc_ref, vc_ref, bias_ref, o_ref):
    nq, nk = Q_ROWS * GRID_W, KEY_ROWS * GRID_W
    kc = kc_ref[...]
    vc = vc_ref[...]
    for g in range(N_QGROUPS):
        variant = 0 if g == 0 else (2 if g == N_QGROUPS - 1 else 1)
        k0 = min(max(g * Q_ROWS - NA_KH // 2, 0), 32 - KEY_ROWS) * GRID_W
        q = q_ref[g * nq:(g + 1) * nq, :]
        s_lat = lax.dot_general(q, k_ref[k0:k0 + nk, :], _NT_DIMS, preferred_element_type=F32)
        s_lat = s_lat + bias_ref[variant]
        s_ctx = lax.dot_general(q, kc, _NT_DIMS, preferred_element_type=F32)
        o = _softmax_av([s_lat, s_ctx], [v_ref[k0:k0 + nk, :], vc])
        o_ref[g * nq:(g + 1) * nq, :] = o.astype(o_ref.dtype)


def neighbourhood_attention(q, k, v, bias, batch, seq, ctx_len):
    dh = NA_HEAD_DIM
    ctx_blk0 = batch * seq // ctx_len
    lat = pl.BlockSpec((seq, dh), lambda h, b: (b, h))
    ctx = pl.BlockSpec((ctx_len, dh), lambda h, b: (ctx_blk0 + b, h))
    return pl.pallas_call(
        _na_body,
        grid=(NA_HEADS, batch),
        in_specs=[lat, lat, lat, ctx, ctx,
                  pl.BlockSpec((None,) + bias.shape[1:], lambda h, b: (h, 0, 0, 0))],
        out_specs=pl.BlockSpec((seq, dh), lambda h, b: (b, h)),
        out_shape=jax.ShapeDtypeStruct((batch * seq, NA_HEADS * dh), BF16),
        compiler_params=_params(("arbitrary", "arbitrary")),
        name="neighbourhood_attention",
    )(q, k, v, k, v, bias)


def _ctx_attn_body(q_ref, k_ref, v_ref, o_ref):
    s = lax.dot_general(q_ref[...], k_ref[...], _NT_DIMS, preferred_element_type=F32)
    o_ref[...] = _softmax_av([s], [v_ref[...]]).astype(o_ref.dtype)


def context_attention(q, k, v, batch, seq, ctx_len):
    dh = NA_HEAD_DIM
    ctx_blk0 = batch * seq // ctx_len
    ctx = pl.BlockSpec((ctx_len, dh), lambda h, b: (ctx_blk0 + b, h))
    return pl.pallas_call(
        _ctx_attn_body,
        grid=(NA_HEADS, batch),
        in_specs=[ctx, ctx, ctx],
        out_specs=pl.BlockSpec((ctx_len, dh), lambda h, b: (b, h)),
        out_shape=jax.ShapeDtypeStruct((batch * ctx_len, NA_HEADS * dh), BF16),
        compiler_params=_params(("arbitrary", "arbitrary")),
        name="context_attention",
    )(q, k, v)


def _sg_mix_body(u_ref, v_ref, vg_ref, ws_ref, bs_ref, o_ref):
    tm = u_ref.shape[0]
    vv = v_ref[...].astype(F32)
    ms = jnp.mean(vv * vv, axis=-1, keepdims=True)
    vn = (vv * lax.rsqrt(ms + EPS) * vg_ref[...]).astype(BF16)
    for c in range(tm // SG_CHUNK):
        rows = slice(c * SG_CHUNK, (c + 1) * SG_CHUNK)
        for g in range(SG_GROUPS):
            cols = slice(g * SG_GROUP_DIM, (g + 1) * SG_GROUP_DIM)
            mixed = jnp.dot(ws_ref[g].astype(BF16), vn[rows, cols], preferred_element_type=F32)
            mixed = mixed + bs_ref[:, g:g + 1]
            o_ref[rows, cols] = (u_ref[rows, cols].astype(F32) * mixed).astype(o_ref.dtype)


def sg_mix(u, v, v_gain, w_s, b_s_t, tm=256):
    n_rows, width = u.shape
    blk = pl.BlockSpec((tm, width), lambda i: (i, 0))
    return pl.pallas_call(
        _sg_mix_body,
        grid=(n_rows // tm,),
        in_specs=[blk, blk,
                  pl.BlockSpec((1, width), lambda i: (0, 0)),
                  pl.BlockSpec(w_s.shape, lambda i: (0, 0, 0)),
                  pl.BlockSpec(b_s_t.shape, lambda i: (0, 0))],
        out_specs=blk,
        out_shape=jax.ShapeDtypeStruct((n_rows, width), BF16),
        compiler_params=_params(("arbitrary",)),
        name="sg_mix",
    )(u, v, v_gain.reshape(1, width), w_s, b_s_t)


def kernel(x, c, ctx, c_ctx, w_ada, b_ada, norm_g, ffn_w_in, ffn_w_out, na_w_qkv, na_w_o, na_q_gain, na_k_gain, na_rpb, sg_w_in, sg_b_in, sg_v_gain, sg_w_s, sg_b_s, sg_w_out):
    batch, seq, d = x.shape
    ctx_len = ctx.shape[1]
    n_lat, n_ctx = batch * seq, batch * ctx_len
    n_all = n_lat + n_ctx
    ctx_row = batch

    c8 = jnp.concatenate([c, c_ctx[None], jnp.zeros((8 - batch - 1, d), F32)], axis=0)
    mods = ada_mods(c8, w_ada, b_ada).reshape(w_ada.shape[0], 8, N_MOD, 1, d)

    def lat_rows(tm):
        return lambda i: jnp.minimum(i * tm // seq, ctx_row)

    def ctx_rows(tm):
        return lambda i: ctx_row

    x2 = x.reshape(n_lat, d)
    z2 = ctx.reshape(n_ctx, d)

    t1 = half_ffn([x2, z2], n_all, 0, mods, 0, 0, lat_rows, norm_g[0, 0], ffn_w_in, ffn_w_out, 0)

    h = norm_modulate([t1], n_all, norm_g[0, 1], mods, 0, 1, lat_rows(512))
    tm, tn = 512, 256
    col = lambda j, i: (i, j)
    qkv_outs = [(jax.ShapeDtypeStruct((n_all, d), BF16), pl.BlockSpec((tm, tn), col))] * 3
    gain_spec = pl.BlockSpec((1, NA_HEAD_DIM), lambda j, i: (0, 0))
    q, k, v = fused_matmul(
        h, n_all, na_w_qkv, (0,), [0, d, 2 * d], d, tm, tn, _qkv_epilogue,
        [(na_q_gain[0].reshape(1, -1), gain_spec), (na_k_gain[0].reshape(1, -1), gain_spec)],
        qkv_outs, "qkv")
    bias = attention_bias(na_rpb[0])
    o_lat = neighbourhood_attention(q, k, v, bias, batch, seq, ctx_len)
    t2 = residual_matmul(o_lat, n_lat, na_w_o, (0,), [t1], 0, mods, 0, 5, lat_rows(1024), 1.0,
                         1024, 512, "attn_out")
    x3 = half_ffn([t2], n_lat, 0, mods, 0, 2, lat_rows, norm_g[0, 2], ffn_w_in, ffn_w_out, 1)

    o_ctx = context_attention(q, k, v, batch, seq, ctx_len)
    z_t2 = residual_matmul(o_ctx, n_ctx, na_w_o, (0,), [t1], n_lat, mods, 0, 5, ctx_rows(1024), 1.0,
                           1024, 512, "attn_out_ctx")
    z3 = half_ffn([z_t2], n_ctx, 0, mods, 0, 2, ctx_rows, norm_g[0, 2], ffn_w_in, ffn_w_out, 1)
    del z3

    x4 = half_ffn([x3], n_lat, 0, mods, 1, 0, lat_rows, norm_g[1, 0], ffn_w_in, ffn_w_out, 0)
    h = norm_modulate([x4], n_lat, norm_g[1, 1], mods, 1, 1, lat_rows(512))
    tm, tn = 512, 512
    b_in = sg_b_in[0].reshape(1, -1)
    bias_spec_u = pl.BlockSpec((1, tn), lambda j, i: (0, j))
    bias_spec_v = pl.BlockSpec((1, tn), lambda j, i: (0, j + d // tn))
    uv_outs = [(jax.ShapeDtypeStruct((n_lat, d), BF16), pl.BlockSpec((tm, tn), col))] * 2
    u, vv = fused_matmul(h, n_lat, sg_w_in, (0,), [0, d], d, tm, tn, _sg_in_epilogue,
                         [(b_in, bias_spec_u), (b_in, bias_spec_v)], uv_outs, "sg_in")
    gated = sg_mix(u, vv, sg_v_gain[0], sg_w_s[0], sg_b_s[0].T)
    x5 = residual_matmul(gated, n_lat, sg_w_out, (0,), [x4], 0, mods, 1, 5, lat_rows(1024), 1.0,
                         1024, 512, "sg_out")
    x6 = half_ffn([x5], n_lat, 0, mods, 1, 2, lat_rows, norm_g[1, 2], ffn_w_in, ffn_w_out, 1)
    return x6.reshape(batch, seq, d)
```

```python
import functools

import jax
import jax.numpy as jnp
from jax import lax
from jax.experimental import pallas as pl
from jax.experimental.pallas import tpu as pltpu

D_MODEL = 4096
GRID_W = 64
NA_HEADS = 32
NA_HEAD_DIM = 128
NA_KH = 8
NA_KW = 16
RPB_H = 15
RPB_W = 31
SG_CHUNK = 128
SG_GROUPS = 8
SG_GROUP_DIM = D_MODEL // SG_GROUPS
N_MOD = 9
EPS = 1e-6
NEG_INF = -1e30

V7X_VMEM_BYTES = 64 * 1024 * 1024
V7X_LANES = 128
VMEM_LIMIT = V7X_VMEM_BYTES - 2 * 1024 * 1024

Q_ROWS = 4
KEY_ROWS = 12
N_QGROUPS = 32 // Q_ROWS
BIAS_VARIANTS = 3

F32 = jnp.float32
BF16 = jnp.bfloat16


def _params(sem, vmem=VMEM_LIMIT):
    return pltpu.CompilerParams(dimension_semantics=sem, vmem_limit_bytes=vmem)


def _sigmoid(a):
    return 1.0 / (1.0 + jnp.exp(-a))


def _ada_body(c_ref, w_ref, b_ref, o_ref):
    cv = c_ref[...]
    a = cv * _sigmoid(cv)
    o_ref[...] = jnp.dot(a, w_ref[...], preferred_element_type=F32) + b_ref[...]


def ada_mods(c8, w_ada, b_ada, tn=1024):
    n_layers, d, n = w_ada.shape
    rows = c8.shape[0]
    return pl.pallas_call(
        _ada_body,
        grid=(n_layers, n // tn),
        in_specs=[
            pl.BlockSpec((rows, d), lambda l, j: (0, 0)),
            pl.BlockSpec((None, d, tn), lambda l, j: (l, 0, j)),
            pl.BlockSpec((None, 1, tn), lambda l, j: (l, 0, j)),
        ],
        out_specs=pl.BlockSpec((None, rows, tn), lambda l, j: (l, 0, j)),
        out_shape=jax.ShapeDtypeStruct((n_layers, rows, n), F32),
        compiler_params=_params(("arbitrary", "arbitrary")),
        name="ada_mods",
    )(c8, w_ada, b_ada.reshape(n_layers, 1, n))


def _mod_spec(layer, k, row_of_block, grid_rank):
    if grid_rank == 1:
        imap = lambda i: (layer, row_of_block(i), k, 0, 0)
    else:
        imap = lambda j, i: (layer, row_of_block(i), k, 0, 0)
    return pl.BlockSpec((None, None, None, 1, D_MODEL), imap)


def _mod_col_spec(layer, k, row_of_block, tn):
    return pl.BlockSpec((None, None, None, 1, tn), lambda j, i: (layer, row_of_block(i), k, 0, j))


def _norm_body(n_first, *refs):
    if n_first is None:
        x_ref, g_ref, sh_ref, sc_ref, o_ref = refs
        xv = x_ref[...]
    else:
        xa_ref, xb_ref, g_ref, sh_ref, sc_ref, o_ref = refs
        xv = jnp.where(pl.program_id(0) < n_first, xa_ref[...], xb_ref[...])
    ms = jnp.mean(xv * xv, axis=-1, keepdims=True)
    y = xv * lax.rsqrt(ms + EPS) * g_ref[...]
    o_ref[...] = (y * (1.0 + sc_ref[...]) + sh_ref[...]).astype(o_ref.dtype)


def norm_modulate(xs, n_rows, gain, mods, layer, sub, row_of_block, tm=512, row_off=0):
    d = D_MODEL
    nb = n_rows // tm
    off = row_off // tm
    if len(xs) == 1:
        n_first = None
        x_specs = [pl.BlockSpec((tm, d), lambda i: (i + off, 0))]
    else:
        n_first = xs[0].shape[0] // tm
        x_specs = [
            pl.BlockSpec((tm, d), lambda i: (jnp.minimum(i, n_first - 1), 0)),
            pl.BlockSpec((tm, d), lambda i: (jnp.maximum(i - n_first, 0), 0)),
        ]
    return pl.pallas_call(
        functools.partial(_norm_body, n_first),
        grid=(nb,),
        in_specs=x_specs + [
            pl.BlockSpec((1, d), lambda i: (0, 0)),
            _mod_spec(layer, 3 * sub, row_of_block, 1),
            _mod_spec(layer, 3 * sub + 1, row_of_block, 1),
        ],
        out_specs=pl.BlockSpec((tm, d), lambda i: (i, 0)),
        out_shape=jax.ShapeDtypeStruct((n_rows, d), BF16),
        compiler_params=_params(("arbitrary",)),
        name="norm_modulate",
    )(*xs, gain.reshape(1, d), mods, mods)


def _mm_body(n_w, n_extra, epilogue, *refs):
    h_ref = refs[0]
    w_refs = refs[1:1 + n_w]
    extra = refs[1 + n_w:1 + n_w + n_extra]
    outs = refs[1 + n_w + n_extra:]
    h = h_ref[...]
    ys = [jnp.dot(h, w_refs[t][...].astype(BF16), preferred_element_type=F32) for t in range(n_w)]
    epilogue(ys, extra, outs)


def fused_matmul(h, n_rows, w, w_prefix, col_offsets, n_cols, tm, tn, epilogue, extras, outs,
                 name, h_row_off=0):
    k = h.shape[1]
    n_w = len(col_offsets)
    h_off = h_row_off // tm
    none_prefix = (None,) * len(w_prefix)
    w_specs = [
        pl.BlockSpec(none_prefix + (k, tn),
                     lambda j, i, o=o // tn: tuple(w_prefix) + (0, j + o))
        for o in col_offsets
    ]
    return pl.pallas_call(
        functools.partial(_mm_body, n_w, len(extras), epilogue),
        grid=(n_cols // tn, n_rows // tm),
        in_specs=[pl.BlockSpec((tm, k), lambda j, i: (i + h_off, 0))] + w_specs
                 + [s for _, s in extras],
        out_specs=[s for _, s in outs],
        out_shape=[a for a, _ in outs],
        compiler_params=_params(("arbitrary", "arbitrary")),
        name=name,
    )(h, *([w] * n_w), *[a for a, _ in extras])


def _swiglu_epilogue(ys, extra, outs):
    a, g = ys
    outs[0][...] = (a * _sigmoid(a) * g).astype(outs[0].dtype)


def _gelu_tanh(v):
    return 0.5 * v * (1.0 + jnp.tanh(0.7978845608028654 * (v + 0.044715 * (v * v * v))))


def _sg_in_epilogue(ys, extra, outs):
    bu_ref, bv_ref = extra
    outs[0][...] = _gelu_tanh(ys[0] + bu_ref[...]).astype(outs[0].dtype)
    outs[1][...] = _gelu_tanh(ys[1] + bv_ref[...]).astype(outs[1].dtype)


def _head_rms(y, gain, mult):
    pieces = []
    for hh in range(y.shape[1] // NA_HEAD_DIM):
        seg = y[:, hh * NA_HEAD_DIM:(hh + 1) * NA_HEAD_DIM]
        ms = jnp.mean(seg * seg, axis=-1, keepdims=True)
        pieces.append(seg * lax.rsqrt(ms + EPS) * gain * mult)
    return pieces


def _qkv_epilogue(ys, extra, outs):
    qg_ref, kg_ref = extra
    q_ref, k_ref, v_ref = outs
    scale = NA_HEAD_DIM ** -0.5
    for hh, seg in enumerate(_head_rms(ys[0], qg_ref[...], scale)):
        q_ref[:, hh * NA_HEAD_DIM:(hh + 1) * NA_HEAD_DIM] = seg.astype(q_ref.dtype)
    for hh, seg in enumerate(_head_rms(ys[1], kg_ref[...], 1.0)):
        k_ref[:, hh * NA_HEAD_DIM:(hh + 1) * NA_HEAD_DIM] = seg.astype(k_ref.dtype)
    v_ref[...] = ys[2].astype(v_ref.dtype)


def _residual_epilogue(coef, n_first, ys, extra, outs):
    if n_first is None:
        gate_ref, x_ref = extra
        xv = x_ref[...]
    else:
        gate_ref, xa_ref, xb_ref = extra
        xv = jnp.where(pl.program_id(1) < n_first, xa_ref[...], xb_ref[...])
    outs[0][...] = xv + (coef * gate_ref[...]) * ys[0]


def residual_matmul(h, n_rows, w, w_prefix, x_olds, x_row_off, mods, layer, gate_k, row_of_block,
                    coef, tm, tn, name, h_row_off=0):
    n = D_MODEL
    if len(x_olds) == 1:
        n_first = None
        xo = x_row_off // tm
        x_specs = [pl.BlockSpec((tm, tn), lambda j, i: (i + xo, j))]
    else:
        n_first = x_olds[0].shape[0] // tm
        x_specs = [
            pl.BlockSpec((tm, tn), lambda j, i: (jnp.minimum(i, n_first - 1), j)),
            pl.BlockSpec((tm, tn), lambda j, i: (jnp.maximum(i - n_first, 0), j)),
        ]
    extras = [(mods, _mod_col_spec(layer, gate_k, row_of_block, tn))]
    extras += list(zip(x_olds, x_specs))
    outs = [(jax.ShapeDtypeStruct((n_rows, n), F32), pl.BlockSpec((tm, tn), lambda j, i: (i, j)))]
    return fused_matmul(h, n_rows, w, w_prefix, [0], n, tm, tn,
                        functools.partial(_residual_epilogue, coef, n_first), extras, outs, name,
                        h_row_off=h_row_off)[0]


def ffn_in(h, n_rows, w_in, w_prefix, tm=1024, tn=512, h_row_off=0):
    d_ff = w_in.shape[-1] // 2
    outs = [(jax.ShapeDtypeStruct((n_rows, d_ff), BF16), pl.BlockSpec((tm, tn), lambda j, i: (i, j)))]
    return fused_matmul(h, n_rows, w_in, w_prefix, [0, d_ff], d_ff, tm, tn, _swiglu_epilogue, [],
                        outs, "ffn_in", h_row_off=h_row_off)[0]


def half_ffn(x_olds, n_rows, x_row_off, mods, layer, sub, row_fn, gain, w_in, w_out, f):
    h = norm_modulate(x_olds, n_rows, gain, mods, layer, sub, row_fn(512), row_off=x_row_off)
    act = ffn_in(h, n_rows, w_in, (layer, f))
    return residual_matmul(act, n_rows, w_out, (layer, f), x_olds, x_row_off, mods, layer,
                           3 * sub + 2, row_fn(512), 0.5, 512, 512, "ffn_out")


def _bias_body(rpb_ref, o_ref):
    hd = pl.program_id(0)
    qc = lax.broadcasted_iota(jnp.int32, (GRID_W, V7X_LANES), 0)
    lane = lax.broadcasted_iota(jnp.int32, (GRID_W, V7X_LANES), 1)
    kc = lane & (GRID_W - 1)
    dc = jnp.clip(kc - qc, -(NA_KW - 1), NA_KW - 1) + NA_KW - 1
    cs = jnp.clip(qc - NA_KW // 2, 0, GRID_W - NA_KW)
    col_ok = (kc >= cs) & (kc < cs + NA_KW)
    left = lane < GRID_W
    neg = jnp.full((GRID_W, V7X_LANES), NEG_INF, F32)

    cw = [jnp.zeros((GRID_W, V7X_LANES), F32) for _ in range(RPB_H)]
    for d in range(RPB_W):
        hit = dc == d
        for dr in range(RPB_H):
            cw[dr] = jnp.where(hit, rpb_ref[hd * (RPB_H * RPB_W) + dr * RPB_W + d], cw[dr])
    cw = [jnp.where(col_ok, t, neg) for t in cw]

    def block(variant, qr, kr):
        if variant == 0:
            ok, dr = 0 <= kr < NA_KH, kr - qr + NA_KH - 1
        elif variant == 1:
            ok, dr = 0 <= kr - qr < NA_KH, kr - qr + NA_KH - 1 - Q_ROWS
        else:
            ok, dr = KEY_ROWS - NA_KH <= kr < KEY_ROWS, kr - qr - 1
        return cw[dr] if ok else None

    for variant in range(BIAS_VARIANTS):
        for qr in range(Q_ROWS):
            for p in range(KEY_ROWS // 2):
                b0, b1 = block(variant, qr, 2 * p), block(variant, qr, 2 * p + 1)
                if b0 is None and b1 is None:
                    blk = neg
                else:
                    blk = jnp.where(left, neg if b0 is None else b0, neg if b1 is None else b1)
                o_ref[variant, qr * GRID_W:(qr + 1) * GRID_W, p * V7X_LANES:(p + 1) * V7X_LANES] = blk


def attention_bias(rpb):
    nq, nk = Q_ROWS * GRID_W, KEY_ROWS * GRID_W
    return pl.pallas_call(
        _bias_body,
        grid=(NA_HEADS,),
        in_specs=[pl.BlockSpec(memory_space=pltpu.SMEM)],
        out_specs=pl.BlockSpec((None, BIAS_VARIANTS, nq, nk), lambda h: (h, 0, 0, 0)),
        out_shape=jax.ShapeDtypeStruct((NA_HEADS, BIAS_VARIANTS, nq, nk), F32),
        compiler_params=_params(("arbitrary",)),
        name="attention_bias",
    )(rpb.reshape(-1))


_NT_DIMS = (((1,), (1,)), ((), ()))


def _softmax_av(s_parts, v_parts):
    m = functools.reduce(jnp.maximum, [jnp.max(s, axis=-1, keepdims=True) for s in s_parts])
    ps = [jnp.exp(s - m) for s in s_parts]
    l = functools.reduce(lambda a, b: a + b, [jnp.sum(p, axis=-1, keepdims=True) for p in ps])
    o = functools.reduce(lambda a, b: a + b, [
        jnp.dot(p.astype(BF16), v, preferred_element_type=F32) for p, v in zip(ps, v_parts)])
    return o / l


def _na_body(q_ref, k_ref, v_ref, kc_ref, vc_ref, bias_ref, o_ref):
    nq, nk = Q_ROWS * GRID_W, KEY_ROWS * GRID_W
    kc = kc_ref[...]
    vc = vc_ref[...]
    for g in range(N_QGROUPS):
        variant = 0 if g == 0 else (2 if g == N_QGROUPS - 1 else 1)
        k0 = min(max(g * Q_ROWS - NA_KH // 2, 0), 32 - KEY_ROWS) * GRID_W
        q = q_ref[g * nq:(g + 1) * nq, :]
        s_lat = lax.dot_general(q, k_ref[k0:k0 + nk, :], _NT_DIMS, preferred_element_type=F32)
        s_lat = s_lat + bias_ref[variant]
        s_ctx = lax.dot_general(q, kc, _NT_DIMS, preferred_element_type=F32)
        o = _softmax_av([s_lat, s_ctx], [v_ref[k0:k0 + nk, :], vc])
        o_ref[g * nq:(g + 1) * nq, :] = o.astype(o_ref.dtype)


def neighbourhood_attention(q, k, v, bias, batch, seq, ctx_len):
    dh = NA_HEAD_DIM
    ctx_blk0 = batch * seq // ctx_len
    lat = pl.BlockSpec((seq, dh), lambda h, b: (b, h))
    ctx = pl.BlockSpec((ctx_len, dh), lambda h, b: (ctx_blk0 + b, h))
    return pl.pallas_call(
        _na_body,
        grid=(NA_HEADS, batch),
        in_specs=[lat, lat, lat, ctx, ctx,
                  pl.BlockSpec((None,) + bias.shape[1:], lambda h, b: (h, 0, 0, 0))],
        out_specs=pl.BlockSpec((seq, dh), lambda h, b: (b, h)),
        out_shape=jax.ShapeDtypeStruct((batch * seq, NA_HEADS * dh), BF16),
        compiler_params=_params(("arbitrary", "arbitrary")),
        name="neighbourhood_attention",
    )(q, k, v, k, v, bias)


def _ctx_attn_body(q_ref, k_ref, v_ref, o_ref):
    s = lax.dot_general(q_ref[...], k_ref[...], _NT_DIMS, preferred_element_type=F32)
    o_ref[...] = _softmax_av([s], [v_ref[...]]).astype(o_ref.dtype)


def context_attention(q, k, v, batch, seq, ctx_len):
    dh = NA_HEAD_DIM
    ctx_blk0 = batch * seq // ctx_len
    ctx = pl.BlockSpec((ctx_len, dh), lambda h, b: (ctx_blk0 + b, h))
    return pl.pallas_call(
        _ctx_attn_body,
        grid=(NA_HEADS, batch),
        in_specs=[ctx, ctx, ctx],
        out_specs=pl.BlockSpec((ctx_len, dh), lambda h, b: (b, h)),
        out_shape=jax.ShapeDtypeStruct((batch * ctx_len, NA_HEADS * dh), BF16),
        compiler_params=_params(("arbitrary", "arbitrary")),
        name="context_attention",
    )(q, k, v)


def _sg_mix_body(u_ref, v_ref, vg_ref, ws_ref, bs_ref, o_ref):
    tm = u_ref.shape[0]
    vv = v_ref[...].astype(F32)
    ms = jnp.mean(vv * vv, axis=-1, keepdims=True)
    vn = (vv * lax.rsqrt(ms + EPS) * vg_ref[...]).astype(BF16)
    for c in range(tm // SG_CHUNK):
        rows = slice(c * SG_CHUNK, (c + 1) * SG_CHUNK)
        for g in range(SG_GROUPS):
            cols = slice(g * SG_GROUP_DIM, (g + 1) * SG_GROUP_DIM)
            mixed = jnp.dot(ws_ref[g].astype(BF16), vn[rows, cols], preferred_element_type=F32)
            mixed = mixed + bs_ref[:, g:g + 1]
            o_ref[rows, cols] = (u_ref[rows, cols].astype(F32) * mixed).astype(o_ref.dtype)


def sg_mix(u, v, v_gain, w_s, b_s_t, tm=256):
    n_rows, width = u.shape
    blk = pl.BlockSpec((tm, width), lambda i: (i, 0))
    return pl.pallas_call(
        _sg_mix_body,
        grid=(n_rows // tm,),
        in_specs=[blk, blk,
                  pl.BlockSpec((1, width), lambda i: (0, 0)),
                  pl.BlockSpec(w_s.shape, lambda i: (0, 0, 0)),
                  pl.BlockSpec(b_s_t.shape, lambda i: (0, 0))],
        out_specs=blk,
        out_shape=jax.ShapeDtypeStruct((n_rows, width), BF16),
        compiler_params=_params(("arbitrary",)),
        name="sg_mix",
    )(u, v, v_gain.reshape(1, width), w_s, b_s_t)


def kernel(x, c, ctx, c_ctx, w_ada, b_ada, norm_g, ffn_w_in, ffn_w_out, na_w_qkv, na_w_o, na_q_gain, na_k_gain, na_rpb, sg_w_in, sg_b_in, sg_v_gain, sg_w_s, sg_b_s, sg_w_out):
    batch, seq, d = x.shape
    ctx_len = ctx.shape[1]
    n_lat, n_ctx = batch * seq, batch * ctx_len
    n_all = n_lat + n_ctx
    ctx_row = batch

    c8 = jnp.concatenate([c, c_ctx[None], jnp.zeros((8 - batch - 1, d), F32)], axis=0)
    mods = ada_mods(c8, w_ada, b_ada).reshape(w_ada.shape[0], 8, N_MOD, 1, d)

    def lat_rows(tm):
        return lambda i: jnp.minimum(i * tm // seq, ctx_row)

    def ctx_rows(tm):
        return lambda i: ctx_row

    x2 = x.reshape(n_lat, d)
    z2 = ctx.reshape(n_ctx, d)

    t1 = half_ffn([x2, z2], n_all, 0, mods, 0, 0, lat_rows, norm_g[0, 0], ffn_w_in, ffn_w_out, 0)

    h = norm_modulate([t1], n_all, norm_g[0, 1], mods, 0, 1, lat_rows(512))
    tm, tn = 1024, 256
    col = lambda j, i: (i, j)
    qkv_outs = [(jax.ShapeDtypeStruct((n_all, d), BF16), pl.BlockSpec((tm, tn), col))] * 3
    gain_spec = pl.BlockSpec((1, NA_HEAD_DIM), lambda j, i: (0, 0))
    q, k, v = fused_matmul(
        h, n_all, na_w_qkv, (0,), [0, d, 2 * d], d, tm, tn, _qkv_epilogue,
        [(na_q_gain[0].reshape(1, -1), gain_spec), (na_k_gain[0].reshape(1, -1), gain_spec)],
        qkv_outs, "qkv")
    bias = attention_bias(na_rpb[0])
    o_lat = neighbourhood_attention(q, k, v, bias, batch, seq, ctx_len)
    t2 = residual_matmul(o_lat, n_lat, na_w_o, (0,), [t1], 0, mods, 0, 5, lat_rows(1024), 1.0,
                         1024, 512, "attn_out")
    x3 = half_ffn([t2], n_lat, 0, mods, 0, 2, lat_rows, norm_g[0, 2], ffn_w_in, ffn_w_out, 1)

    o_ctx = context_attention(q, k, v, batch, seq, ctx_len)
    z_t2 = residual_matmul(o_ctx, n_ctx, na_w_o, (0,), [t1], n_lat, mods, 0, 5, ctx_rows(1024), 1.0,
                           1024, 512, "attn_out_ctx")
    z3 = half_ffn([z_t2], n_ctx, 0, mods, 0, 2, ctx_rows, norm_g[0, 2], ffn_w_in, ffn_w_out, 1)
    del z3

    x4 = half_ffn([x3], n_lat, 0, mods, 1, 0, lat_rows, norm_g[1, 0], ffn_w_in, ffn_w_out, 0)
    h = norm_modulate([x4], n_lat, norm_g[1, 1], mods, 1, 1, lat_rows(512))
    tm, tn = 1024, 512
    b_in = sg_b_in[0].reshape(1, -1)
    bias_spec_u = pl.BlockSpec((1, tn), lambda j, i: (0, j))
    bias_spec_v = pl.BlockSpec((1, tn), lambda j, i: (0, j + d // tn))
    uv_outs = [(jax.ShapeDtypeStruct((n_lat, d), BF16), pl.BlockSpec((tm, tn), col))] * 2
    u, vv = fused_matmul(h, n_lat, sg_w_in, (0,), [0, d], d, tm, tn, _sg_in_epilogue,
                         [(b_in, bias_spec_u), (b_in, bias_spec_v)], uv_outs, "sg_in")
    gated = sg_mix(u, vv, sg_v_gain[0], sg_w_s[0], sg_b_s[0].T)
    x5 = residual_matmul(gated, n_lat, sg_w_out, (0,), [x4], 0, mods, 1, 5, lat_rows(1024), 1.0,
                         1024, 512, "sg_out")
    x6 = half_ffn([x5], n_lat, 0, mods, 1, 2, lat_rows, norm_g[1, 2], ffn_w_in, ffn_w_out, 1)
    return x6.reshape(batch, seq, d)
```
